```python
import jax, jax.numpy as jnp
from jax import lax
import numpy as np

D_MODEL = 1024
BATCH = 4
SEQ = 4096
DEPTH = 1
DEC_BATCH = 32
DEC_SEQ = 4
PAST_LEN = 8192
PAGE_SIZE = 128

N_HEADS = 8
HEAD_DIM = 64
N_KV_HEADS = 2
GROUP = N_HEADS // N_KV_HEADS
ROT_DIM = HEAD_DIM // 4
ROPE_THETA = 500000.0
CMP_BLOCK = 32
CMP_STRIDE = 16
CMP_HIDDEN = 128
SEL_BLOCK = 64
N_SEL = 16
WINDOW = 512
FORCED_SCORE = 1.0e4
CONV_CH = 512
CONV_WIDTH = 31
N_MEM = 256
X_HEADS = 4
X_HEAD_DIM = 128
D_FF = 4 * D_MODEL
QUERY_BLOCK = 128
EPS = 1e-6
Q_W = N_HEADS * HEAD_DIM
KV_W = N_KV_HEADS * HEAD_DIM
X_W = X_HEADS * X_HEAD_DIM
IN_W = Q_W + 6 * KV_W + 3 * N_HEADS + 2 * CONV_CH + 2 * D_MODEL

kernel_name = 'nsa_conformer_gated_hybrid_step'


def rms_norm(x, g):
    xf = x.astype(jnp.float32)
    y = xf * lax.rsqrt(jnp.mean(xf * xf, axis=-1, keepdims=True) + EPS)
    return (y * g.astype(jnp.float32)).astype(x.dtype)


def layer_norm(x, g, b):
    xf = x.astype(jnp.float32)
    xc = xf - jnp.mean(xf, axis=-1, keepdims=True)
    var = jnp.mean(xc * xc, axis=-1, keepdims=True)
    return (xc * lax.rsqrt(var + EPS) * g.astype(jnp.float32) + b.astype(jnp.float32)).astype(x.dtype)


def rope(x, pos):
    half = ROT_DIM // 2
    inv = ROPE_THETA ** (-jnp.arange(half, dtype=jnp.float32) / half)
    ang = pos.astype(jnp.float32)[:, None] * inv[None, :]
    cos = jnp.cos(ang)[:, None, :]
    sin = jnp.sin(ang)[:, None, :]
    xr = x[..., :ROT_DIM].astype(jnp.float32)
    x1, x2 = xr[..., :half], xr[..., half:]
    rot = jnp.concatenate([x1 * cos - x2 * sin, x2 * cos + x1 * sin], axis=-1)
    return jnp.concatenate([rot.astype(x.dtype), x[..., ROT_DIM:]], axis=-1)


def masked_softmax(s, mask):
    s = s.astype(jnp.float32)
    m = jnp.max(jnp.where(mask, s, -jnp.inf), axis=-1, keepdims=True)
    m = jnp.where(jnp.isfinite(m), m, 0.0)
    e = jnp.where(mask, jnp.exp(s - m), 0.0)
    return e / jnp.maximum(jnp.sum(e, axis=-1, keepdims=True), 1e-30)


def split_points():
    sizes = [Q_W] + [KV_W] * 6 + [3 * N_HEADS, 2 * CONV_CH, 2 * D_MODEL]
    return [int(v) for v in np.cumsum(sizes)[:-1]]


def mixer_inputs(h, w_in, pos):
    bsz, L = h.shape[0], h.shape[1]
    q, k_cmp, v_cmp, k_slc, v_slc, k_win, v_win, gates, glu_in, merge = jnp.split(h @ w_in, split_points(), axis=-1)
    kvh = lambda a: a.reshape(bsz, L, N_KV_HEADS, HEAD_DIM)
    q = rope(q.reshape(bsz, L, N_HEADS, HEAD_DIM), pos).reshape(bsz, L, N_KV_HEADS, GROUP, HEAD_DIM)
    k_cmp = rope(kvh(k_cmp), pos)
    k_slc = rope(kvh(k_slc), pos)
    k_win = rope(kvh(k_win), pos)
    gates = gates.reshape(bsz, L, N_KV_HEADS, GROUP, 3)
    a, b = jnp.split(glu_in, 2, axis=-1)
    glu = a * jax.nn.sigmoid(b)
    return q, k_cmp, kvh(v_cmp), k_slc, kvh(v_slc), k_win, kvh(v_win), gates, glu, merge


def compress(k, pos_emb, w1, w2):
    bsz, L = k.shape[0], k.shape[1]
    nc = (L - CMP_BLOCK) // CMP_STRIDE + 1
    r = CMP_BLOCK // CMP_STRIDE
    chunks = k[:, :(nc + r - 1) * CMP_STRIDE].reshape(bsz, nc + r - 1, CMP_STRIDE, N_KV_HEADS, HEAD_DIM)
    blocks = jnp.concatenate([chunks[:, i:i + nc] for i in range(r)], axis=2)
    blocks = blocks + pos_emb[None, None, :, None, :]
    flat = blocks.transpose(0, 1, 3, 2, 4).reshape(bsz, nc, N_KV_HEADS, CMP_BLOCK * HEAD_DIM)
    return jax.nn.gelu(flat @ w1) @ w2


def cmp_to_sel(nc, ns):
    start = np.arange(nc)[:, None] * CMP_STRIDE
    sel0 = np.arange(ns)[None, :] * SEL_BLOCK
    ov = np.clip(np.minimum(start + CMP_BLOCK, sel0 + SEL_BLOCK) - np.maximum(start, sel0), 0, None)
    return jnp.asarray(ov / CMP_BLOCK, dtype=jnp.float32)


def to_sel_blocks(k):
    bsz, L = k.shape[0], k.shape[1]
    ns = -(-L // SEL_BLOCK)
    k = jnp.pad(k, ((0, 0), (0, ns * SEL_BLOCK - L), (0, 0), (0, 0)))
    return k.reshape(bsz, ns, SEL_BLOCK, N_KV_HEADS, HEAD_DIM).transpose(0, 3, 1, 2, 4)


def gather_pages(pool, page_table):
    g = pool[page_table]
    return g.reshape(g.shape[0], g.shape[1] * g.shape[2], N_KV_HEADS, HEAD_DIM)


def nsa_attend(q, t, kc, vc, ks, vs, kw, vw, pw, gates):
    scale = HEAD_DIM ** -0.5
    bsz, nq = q.shape[0], q.shape[1]
    nc, ns = kc.shape[1], ks.shape[2]
    c_end = jnp.arange(nc, dtype=jnp.int32) * CMP_STRIDE + (CMP_BLOCK - 1)
    c_mask = (c_end[None, :] <= t[:, None])[None, :, None, None, :]
    p_c = masked_softmax(jnp.einsum('bqhgd,bchd->bqhgc', q, kc) * scale, c_mask)
    o_c = jnp.einsum('bqhgc,bchd->bqhgd', p_c.astype(vc.dtype), vc)
    imp = jnp.einsum('bqhgc,cn->bqhn', p_c, cmp_to_sel(nc, ns))
    j = jnp.arange(ns, dtype=jnp.int32)[None, :]
    cur = (t // SEL_BLOCK)[:, None]
    forced = (j == 0) | (j == cur) | (j == cur - 1)
    eligible = j * SEL_BLOCK <= t[:, None]
    imp = jnp.where(forced[None, :, None, :], FORCED_SCORE, imp)
    imp = jnp.where(eligible[None, :, None, :], imp, -1.0)
    n_top = min(N_SEL, ns)
    _, idx = lax.top_k(imp, n_top)
    bi = jnp.arange(bsz)[:, None, None, None]
    hi = jnp.arange(N_KV_HEADS)[None, None, :, None]
    n_keys = n_top * SEL_BLOCK
    k_sel = ks[bi, hi, idx].reshape(bsz, nq, N_KV_HEADS, n_keys, HEAD_DIM)
    v_sel = vs[bi, hi, idx].reshape(bsz, nq, N_KV_HEADS, n_keys, HEAD_DIM)
    sel_pos = (idx[..., None] * SEL_BLOCK + jnp.arange(SEL_BLOCK, dtype=jnp.int32)).reshape(bsz, nq, N_KV_HEADS, n_keys)
    s_mask = (sel_pos <= t[None, :, None, None])[:, :, :, None, :]
    p_s = masked_softmax(jnp.einsum('bqhgd,bqhkd->bqhgk', q, k_sel) * scale, s_mask)
    o_s = jnp.einsum('bqhgk,bqhkd->bqhgd', p_s.astype(v_sel.dtype), v_sel)
    dt = t[:, None] - pw[None, :]
    w_mask = ((dt >= 0) & (dt < WINDOW) & (pw[None, :] >= 0))[None, :, None, None, :]
    p_w = masked_softmax(jnp.einsum('bqhgd,bkhd->bqhgk', q, kw) * scale, w_mask)
    o_w = jnp.einsum('bqhgk,bkhd->bqhgd', p_w.astype(vw.dtype), vw)
    g = jax.nn.sigmoid(gates.astype(jnp.float32)).astype(q.dtype)
    return g[..., 0:1] * o_c + g[..., 1:2] * o_s + g[..., 2:3] * o_w


def conv_tail(u, w_dw, b_dw, ln_g, ln_b, w_pw):
    c = lax.conv_general_dilated(u, w_dw[:, None, :], window_strides=(1,), padding='VALID',
                                 dimension_numbers=('NWC', 'WIO', 'NWC'), feature_group_count=CONV_CH)
    c = layer_norm(c + b_dw, ln_g, ln_b)
    return jax.nn.silu(c) @ w_pw


def merge_out(y_nsa, y_conv, merge, w_out):
    gm = jax.nn.sigmoid(merge.astype(jnp.float32)).astype(y_nsa.dtype)
    return (gm[..., :D_MODEL] * y_nsa + gm[..., D_MODEL:] * y_conv) @ w_out


def cross_attend(h, mk, mv, w_xq, w_xo):
    bsz, L = h.shape[0], h.shape[1]
    q = (h @ w_xq).reshape(bsz, L, X_HEADS, X_HEAD_DIM)
    s = jnp.einsum('bqhd,bmhd->bhqm', q, mk) * (X_HEAD_DIM ** -0.5)
    p = jax.nn.softmax(s.astype(jnp.float32), axis=-1).astype(h.dtype)
    return jnp.einsum('bhqm,bmhd->bqhd', p, mv).reshape(bsz, L, X_W) @ w_xo


def sq_relu_mlp(h, w_up, w_down):
    u = jax.nn.relu(h @ w_up)
    return (u * u) @ w_down


def prompt_layer(x, mem, w):
    (norm_mix, w_in, cmp_pos_k, cmp_pos_v, w_ck1, w_ck2, w_cv1, w_cv2, w_nsa_o, w_dw, b_dw, conv_ln_g, conv_ln_b,
     w_pw, w_out, norm_x, norm_mem, w_xq, w_xk, w_xv, w_xo, norm_ff, w_up, w_down) = w
    bsz, S = x.shape[0], x.shape[1]
    pos = jnp.arange(S, dtype=jnp.int32)
    h = rms_norm(x, norm_mix)
    q, k_cmp, v_cmp, k_slc, v_slc, k_win, v_win, gates, glu, merge = mixer_inputs(h, w_in, pos)
    kcb = compress(k_cmp, cmp_pos_k, w_ck1, w_ck2)
    vcb = compress(v_cmp, cmp_pos_v, w_cv1, w_cv2)
    ksb = to_sel_blocks(k_slc)
    vsb = to_sel_blocks(v_slc)
    pad = ((0, 0), (WINDOW, 0), (0, 0), (0, 0))
    kw_pad = jnp.pad(k_win, pad)
    vw_pad = jnp.pad(v_win, pad)

    def q_block(i):
        s0 = i * QUERY_BLOCK
        qb = lax.dynamic_slice_in_dim(q, s0, QUERY_BLOCK, axis=1)
        gb = lax.dynamic_slice_in_dim(gates, s0, QUERY_BLOCK, axis=1)
        kwb = lax.dynamic_slice_in_dim(kw_pad, s0, WINDOW + QUERY_BLOCK, axis=1)
        vwb = lax.dynamic_slice_in_dim(vw_pad, s0, WINDOW + QUERY_BLOCK, axis=1)
        tb = s0 + jnp.arange(QUERY_BLOCK, dtype=jnp.int32)
        pwb = s0 - WINDOW + jnp.arange(WINDOW + QUERY_BLOCK, dtype=jnp.int32)
        return nsa_attend(qb, tb, kcb, vcb, ksb, vsb, kwb, vwb, pwb, gb)

    o = lax.map(q_block, jnp.arange(S // QUERY_BLOCK, dtype=jnp.int32))
    o = jnp.moveaxis(o, 0, 1).reshape(bsz, S, Q_W)
    u = jnp.pad(glu, ((0, 0), (CONV_WIDTH - 1, 0), (0, 0)))
    y_conv = conv_tail(u, w_dw, b_dw, conv_ln_g, conv_ln_b, w_pw)
    x = x + merge_out(o @ w_nsa_o, y_conv, merge, w_out)
    mn = rms_norm(mem, norm_mem)
    mk = (mn @ w_xk).reshape(bsz, mem.shape[1], X_HEADS, X_HEAD_DIM)
    mv = (mn @ w_xv).reshape(bsz, mem.shape[1], X_HEADS, X_HEAD_DIM)
    x = x + cross_attend(rms_norm(x, norm_x), mk, mv, w_xq, w_xo)
    x = x + sq_relu_mlp(rms_norm(x, norm_ff), w_up, w_down)
    wb = min(WINDOW, S)
    return x, (k_cmp, v_cmp, k_slc, v_slc, k_win[:, S - wb:], v_win[:, S - wb:], glu[:, S - (CONV_WIDTH - 1):], mk, mv)


def sample_layer(x, ck_cmp, cv_cmp, ck_slc, cv_slc, ck_win, cv_win, s_conv, cm_k, cm_v, page_table, w):
    (norm_mix, w_in, cmp_pos_k, cmp_pos_v, w_ck1, w_ck2, w_cv1, w_cv2, w_nsa_o, w_dw, b_dw, conv_ln_g, conv_ln_b,
     w_pw, w_out, norm_x, norm_mem, w_xq, w_xk, w_xv, w_xo, norm_ff, w_up, w_down) = w
    bsz, T = x.shape[0], x.shape[1]
    pos = PAST_LEN + jnp.arange(T, dtype=jnp.int32)
    h = rms_norm(x, norm_mix)
    q, k_cmp, v_cmp, k_slc, v_slc, k_win, v_win, gates, glu, merge = mixer_inputs(h, w_in, pos)
    full = lambda pool, new: jnp.concatenate([gather_pages(pool, page_table), new], axis=1)
    kcb = compress(full(ck_cmp, k_cmp), cmp_pos_k, w_ck1, w_ck2)
    vcb = compress(full(cv_cmp, v_cmp), cmp_pos_v, w_cv1, w_cv2)
    ksb = to_sel_blocks(full(ck_slc, k_slc))
    vsb = to_sel_blocks(full(cv_slc, v_slc))
    wb = ck_win.shape[1]
    kw = jnp.concatenate([ck_win, k_win], axis=1)
    vw = jnp.concatenate([cv_win, v_win], axis=1)
    pw = PAST_LEN - wb + jnp.arange(wb + T, dtype=jnp.int32)
    o = nsa_attend(q, pos, kcb, vcb, ksb, vsb, kw, vw, pw, gates).reshape(bsz, T, Q_W)
    u = jnp.concatenate([s_conv, glu], axis=1)
    y_conv = conv_tail(u, w_dw, b_dw, conv_ln_g, conv_ln_b, w_pw)
    x = x + merge_out(o @ w_nsa_o, y_conv, merge, w_out)
    x = x + cross_attend(rms_norm(x, norm_x), cm_k, cm_v, w_xq, w_xo)
    x = x + sq_relu_mlp(rms_norm(x, norm_ff), w_up, w_down)
    return x, (k_cmp, v_cmp, k_slc, v_slc, kw[:, T:], vw[:, T:], u[:, T:])


def setup_inputs(seed: int = 0) -> dict:
    key = jax.random.key(seed)
    keys = iter(jax.random.split(key, 48))
    nrm = lambda shape, scale: jax.random.normal(next(keys), shape, jnp.float32) * scale
    gain = lambda shape: 1.0 + nrm(shape, 0.01)
    n_pages = PAST_LEN // PAGE_SIZE
    n_used = DEC_BATCH * n_pages
    n_pool = n_used + (n_used + 3) // 4
    wb = min(WINDOW, PAST_LEN)
    pool_shape = (DEPTH, n_pool, PAGE_SIZE, N_KV_HEADS, HEAD_DIM)
    page_table = jax.random.permutation(next(keys), n_pool)[:n_used].reshape(DEC_BATCH, n_pages).astype(jnp.int32)
    L_ = DEPTH
    return {
        'x_prompt': nrm((BATCH, SEQ, D_MODEL), 1.0),
        'x_sample': nrm((DEC_BATCH, DEC_SEQ, D_MODEL), 1.0),
        'mem_prompt': nrm((BATCH, N_MEM, D_MODEL), 1.0),
        'cache_k_cmp': nrm(pool_shape, 1.0),
        'cache_v_cmp': nrm(pool_shape, 1.0),
        'cache_k_slc': nrm(pool_shape, 1.0),
        'cache_v_slc': nrm(pool_shape, 1.0),
        'cache_k_win': nrm((DEPTH, DEC_BATCH, wb, N_KV_HEADS, HEAD_DIM), 1.0),
        'cache_v_win': nrm((DEPTH, DEC_BATCH, wb, N_KV_HEADS, HEAD_DIM), 1.0),
        'state_conv': nrm((DEPTH, DEC_BATCH, CONV_WIDTH - 1, CONV_CH), 0.5),
        'cache_mem_k': nrm((DEPTH, DEC_BATCH, N_MEM, X_HEADS, X_HEAD_DIM), 1.0),
        'cache_mem_v': nrm((DEPTH, DEC_BATCH, N_MEM, X_HEADS, X_HEAD_DIM), 1.0),
        'page_table': page_table,
        'norm_mix': gain((L_, D_MODEL)),
        'w_in': nrm((L_, D_MODEL, IN_W), D_MODEL ** -0.5),
        'cmp_pos_k': nrm((L_, CMP_BLOCK, HEAD_DIM), 0.02),
        'cmp_pos_v': nrm((L_, CMP_BLOCK, HEAD_DIM), 0.02),
        'w_ck1': nrm((L_, CMP_BLOCK * HEAD_DIM, CMP_HIDDEN), (CMP_BLOCK * HEAD_DIM) ** -0.5),
        'w_ck2': nrm((L_, CMP_HIDDEN, HEAD_DIM), CMP_HIDDEN ** -0.5),
        'w_cv1': nrm((L_, CMP_BLOCK * HEAD_DIM, CMP_HIDDEN), (CMP_BLOCK * HEAD_DIM) ** -0.5),
        'w_cv2': nrm((L_, CMP_HIDDEN, HEAD_DIM), CMP_HIDDEN ** -0.5),
        'w_nsa_o': nrm((L_, Q_W, D_MODEL), Q_W ** -0.5),
        'w_dw': nrm((L_, CONV_WIDTH, CONV_CH), CONV_WIDTH ** -0.5),
        'b_dw': nrm((L_, CONV_CH), 0.01),
        'conv_ln_g': gain((L_, CONV_CH)),
        'conv_ln_b': nrm((L_, CONV_CH), 0.01),
        'w_pw': nrm((L_, CONV_CH, D_MODEL), CONV_CH ** -0.5),
        'w_out': nrm((L_, D_MODEL, D_MODEL), D_MODEL ** -0.5),
        'norm_x': gain((L_, D_MODEL)),
        'norm_mem': gain((L_, D_MODEL)),
        'w_xq': nrm((L_, D_MODEL, X_W), D_MODEL ** -0.5),
        'w_xk': nrm((L_, D_MODEL, X_W), D_MODEL ** -0.5),
        'w_xv': nrm((L_, D_MODEL, X_W), D_MODEL ** -0.5),
        'w_xo': nrm((L_, X_W, D_MODEL), X_W ** -0.5),
        'norm_ff': gain((L_, D_MODEL)),
        'w_up': nrm((L_, D_MODEL, D_FF), D_MODEL ** -0.5),
        'w_down': nrm((L_, D_FF, D_MODEL), D_FF ** -0.5),
        'norm_final': gain((D_MODEL,)),
    }


def reference(x_prompt, x_sample, mem_prompt, cache_k_cmp, cache_v_cmp, cache_k_slc, cache_v_slc, cache_k_win,
              cache_v_win, state_conv, cache_mem_k, cache_mem_v, page_table, norm_mix, w_in, cmp_pos_k, cmp_pos_v,
              w_ck1, w_ck2, w_cv1, w_cv2, w_nsa_o, w_dw, b_dw, conv_ln_g, conv_ln_b, w_pw, w_out, norm_x, norm_mem,
              w_xq, w_xk, w_xv, w_xo, norm_ff, w_up, w_down, norm_final):
    xp, xs = x_prompt, x_sample
    p_states, s_states = [], []
    for l in range(DEPTH):
        w = (norm_mix[l], w_in[l], cmp_pos_k[l], cmp_pos_v[l], w_ck1[l], w_ck2[l], w_cv1[l], w_cv2[l], w_nsa_o[l],
             w_dw[l], b_dw[l], conv_ln_g[l], conv_ln_b[l], w_pw[l], w_out[l], norm_x[l], norm_mem[l], w_xq[l],
             w_xk[l], w_xv[l], w_xo[l], norm_ff[l], w_up[l], w_down[l])
        xp, st = prompt_layer(xp, mem_prompt, w)
        p_states.append(st)
        xs, st = sample_layer(xs, cache_k_cmp[l], cache_v_cmp[l], cache_k_slc[l], cache_v_slc[l], cache_k_win[l],
                              cache_v_win[l], state_conv[l], cache_mem_k[l], cache_mem_v[l], page_table, w)
        s_states.append(st)
    y_prompt = rms_norm(xp, norm_final)
    y_sample = rms_norm(xs, norm_final)
    (pk_cmp, pv_cmp, pk_slc, pv_slc, pk_win, pv_win, p_conv, p_mem_k, p_mem_v) = [jnp.stack(a) for a in zip(*p_states)]
    (sk_cmp, sv_cmp, sk_slc, sv_slc, sk_win, sv_win, s_conv) = [jnp.stack(a) for a in zip(*s_states)]
    return (y_prompt, y_sample, pk_cmp, pv_cmp, pk_slc, pv_slc, pk_win, pv_win, p_conv, p_mem_k, p_mem_v,
            sk_cmp, sv_cmp, sk_slc, sv_slc, sk_win, sv_win, s_conv)
```

```python
import functools

import numpy as np
import jax
import jax.numpy as jnp
from jax import lax
from jax.experimental import pallas as pl
from jax.experimental.pallas import tpu as pltpu

F32 = jnp.float32
BF16 = jnp.bfloat16

D_MODEL = 1024
SEQ = 4096
DEC_SEQ = 4
PAST_LEN = 8192
PAGE_SIZE = 128
N_HEADS = 8
HEAD_DIM = 64
N_KV_HEADS = 2
GROUP = N_HEADS // N_KV_HEADS
ROT_DIM = HEAD_DIM // 4
ROPE_THETA = 500000.0
CMP_BLOCK = 32
CMP_STRIDE = 16
CMP_HIDDEN = 128
SEL_BLOCK = 64
N_SEL = 16
WINDOW = 512
FORCED_SCORE = 1.0e4
CONV_CH = 512
CONV_WIDTH = 31
N_MEM = 256
X_HEADS = 4
X_HEAD_DIM = 128
D_FF = 4 * D_MODEL
QUERY_BLOCK = 128
EPS = 1e-6
Q_W = N_HEADS * HEAD_DIM
KV_W = N_KV_HEADS * HEAD_DIM
X_W = X_HEADS * X_HEAD_DIM

LANES = 128
NEG_BIG = -1.0e30
VMEM_LIMIT = 56 * 1024 * 1024


def _cparams(n_axes):
    return pltpu.CompilerParams(dimension_semantics=("arbitrary",) * n_axes,
                                vmem_limit_bytes=VMEM_LIMIT)


def _dot(a, b):
    return jnp.dot(a, b, preferred_element_type=F32)


def _dot_nt(a, b):
    return lax.dot_general(a, b, (((1,), (1,)), ((), ())), preferred_element_type=F32)


def _rms(x, g):
    return x * lax.rsqrt(jnp.mean(x * x, axis=-1, keepdims=True) + EPS) * g


def _split3(x):
    x1 = x.astype(BF16)
    r1 = x - x1.astype(F32)
    x2 = r1.astype(BF16)
    x3 = (r1 - x2.astype(F32)).astype(BF16)
    return x1, x2, x3


def _dot_exact_lhs(x, m_bf):
    x1, x2, x3 = _split3(x)
    return _dot(x1, m_bf) + _dot(x2, m_bf) + _dot(x3, m_bf)


def _dot_exact_rhs(m_bf, x):
    x1, x2, x3 = _split3(x)
    return _dot(m_bf, x1) + _dot(m_bf, x2) + _dot(m_bf, x3)


def _masked_softmax_parts(parts):
    m = None
    for s, mk in parts:
        mi = jnp.max(jnp.where(mk, s, NEG_BIG), axis=-1, keepdims=True)
        m = mi if m is None else jnp.maximum(m, mi)
    es = []
    den = None
    for s, mk in parts:
        e = jnp.where(mk, jnp.exp(s - m), 0.0)
        es.append(e)
        d = jnp.sum(e, axis=-1, keepdims=True)
        den = d if den is None else den + d
    inv = 1.0 / jnp.maximum(den, 1e-30)
    return [e * inv for e in es]


def _topk_select(imp, n_real, n_top):
    lane = lax.broadcasted_iota(jnp.int32, imp.shape, 1)
    cnt = jnp.zeros(imp.shape, F32)
    for i in range(n_real):
        col = jnp.broadcast_to(imp[:, i:i + 1], imp.shape)
        tie = jnp.where(lane > i, 1.0, 0.0)
        cnt = cnt + jnp.where(col > imp, 1.0, jnp.where(col == imp, tie, 0.0))
    return jnp.where((cnt < float(n_top)) & (lane < n_real), 1.0, 0.0)


def _importance(imp, t_row, n_real):
    j = lax.broadcasted_iota(jnp.int32, imp.shape, 1)
    cur = lax.shift_right_logical(t_row, 6)
    forced = (j == 0) | (j == cur) | (j == cur - 1)
    eligible = (j * SEL_BLOCK <= t_row) & (j < n_real)
    imp = jnp.where(forced, FORCED_SCORE, imp)
    return jnp.where(eligible, imp, -1.0)


PROJ_W = Q_W + 6 * KV_W + LANES + 2 * CONV_CH


def _proj_kernel(x_ref, g_ref, w_ref, cos_ref, sa_ref, sb_ref,
                 q_ref, kc_ref, vc_ref, ks_ref, vs_ref, kw_ref, vw_ref, gate_ref, glu_ref):
    h = _rms(x_ref[...], g_ref[...])
    z = _dot(h.astype(BF16), w_ref[...])
    cos, sa, sb = cos_ref[...], sa_ref[...], sb_ref[...]

    def rope(blk):
        return blk * cos + pltpu.roll(blk, LANES - 8, 1) * sa + pltpu.roll(blk, 8, 1) * sb

    for g in range(GROUP):
        q_ref[:, g * LANES:(g + 1) * LANES] = rope(z[:, g * LANES:(g + 1) * LANES])
    o = Q_W
    kc_ref[...] = rope(z[:, o:o + KV_W])
    vc_ref[...] = z[:, o + KV_W:o + 2 * KV_W]
    ks_ref[...] = rope(z[:, o + 2 * KV_W:o + 3 * KV_W])
    vs_ref[...] = z[:, o + 3 * KV_W:o + 4 * KV_W]
    kw_ref[...] = rope(z[:, o + 4 * KV_W:o + 5 * KV_W])
    vw_ref[...] = z[:, o + 5 * KV_W:o + 6 * KV_W]
    o += 6 * KV_W
    gate_ref[...] = jax.nn.sigmoid(z[:, o:o + LANES])
    o += LANES
    a = z[:, o:o + CONV_CH]
    b = z[:, o + CONV_CH:o + 2 * CONV_CH]
    glu_ref[...] = a * jax.nn.sigmoid(b)


def _proj(x, gain, w_proj, cos, sa, sb, tm, pos_tiles):
    n = x.shape[0]
    row = lambda w: pl.BlockSpec((tm, w), lambda i: (i, 0))
    const = lambda a: pl.BlockSpec(a.shape, lambda i: (0,) * a.ndim)
    tab = pl.BlockSpec((tm, LANES), lambda i: (i % pos_tiles, 0))
    widths = [Q_W] + [KV_W] * 6 + [LANES, CONV_CH]
    return pl.pallas_call(
        _proj_kernel,
        grid=(n // tm,),
        in_specs=[row(D_MODEL), const(gain), const(w_proj), tab, tab, tab],
        out_specs=[row(w) for w in widths],
        out_shape=[jax.ShapeDtypeStruct((n, w), F32) for w in widths],
        compiler_params=_cparams(1),
        name="in_proj",
    )(x, gain, w_proj, cos, sa, sb)


def _compress_core(load_p, pe_ref, wt_ref, wb_ref, w2_ref, nrows):
    n_chunk = CMP_BLOCK // 2
    a = jnp.zeros((nrows, 2 * CMP_HIDDEN), F32)
    b = jnp.zeros((nrows, 2 * CMP_HIDDEN), F32)
    for p in range(n_chunk):
        xp = load_p(p)
        a = a + _dot((xp + pe_ref[p:p + 1, :]).astype(BF16), wt_ref[p])
        b = b + _dot((xp + pe_ref[n_chunk + p:n_chunk + p + 1, :]).astype(BF16), wb_ref[p])
    pre = a + pltpu.roll(b, nrows - 1, 0)
    hid = jax.nn.gelu(pre)
    return _dot(hid.astype(BF16), w2_ref[...])


def _compress_prompt_kernel(k_ref, v_ref, pek_ref, wtk_ref, wbk_ref, w2k_ref,
                            pev_ref, wtv_ref, wbv_ref, w2v_ref, kc_ref, vc_ref):
    nrows = k_ref.shape[1] // CMP_STRIDE
    kc_ref[0] = _compress_core(lambda p: k_ref[0, pl.ds(p, nrows, stride=CMP_STRIDE), :],
                               pek_ref, wtk_ref, wbk_ref, w2k_ref, nrows)
    vc_ref[0] = _compress_core(lambda p: v_ref[0, pl.ds(p, nrows, stride=CMP_STRIDE), :],
                               pev_ref, wtv_ref, wbv_ref, w2v_ref, nrows)


def _compress_prompt(k, v, cw_k, cw_v):
    bsz, L, _ = k.shape
    nrows = L // CMP_STRIDE
    tok = pl.BlockSpec((1, L, KV_W), lambda b: (b, 0, 0))
    const = lambda a: pl.BlockSpec(a.shape, lambda b: (0,) * a.ndim)
    out = pl.BlockSpec((1, nrows, KV_W), lambda b: (b, 0, 0))
    return pl.pallas_call(
        _compress_prompt_kernel,
        grid=(bsz,),
        in_specs=[tok, tok] + [const(a) for a in cw_k + cw_v],
        out_specs=[out, out],
        out_shape=[jax.ShapeDtypeStruct((bsz, nrows, KV_W), F32)] * 2,
        compiler_params=_cparams(1),
        name="compress_prompt",
    )(k, v, *cw_k, *cw_v)


def _page_copies(pool_ref, pt_ref, slab_ref, sem_ref, b, slot, n_pages):
    return [pltpu.make_async_copy(pool_ref.at[pt_ref[b, j]],
                                  slab_ref.at[slot, pl.ds(j * PAGE_SIZE, PAGE_SIZE)],
                                  sem_ref.at[slot])
            for j in range(n_pages)]


def _compress_sample_kernel(pt_ref, kpool_ref, vpool_ref, pek_ref, wtk_ref, wbk_ref, w2k_ref,
                            pev_ref, wtv_ref, wbv_ref, w2v_ref, kc_ref, vc_ref,
                            kslab, vslab, ksem, vsem):
    b = pl.program_id(0)
    nb = pl.num_programs(0)
    n_pages = pt_ref.shape[1]
    nrows = n_pages * PAGE_SIZE // CMP_STRIDE
    slot = b % 2

    def start(bb, sl):
        for c in _page_copies(kpool_ref, pt_ref, kslab, ksem, bb, sl, n_pages):
            c.start()
        for c in _page_copies(vpool_ref, pt_ref, vslab, vsem, bb, sl, n_pages):
            c.start()

    @pl.when(b == 0)
    def _():
        start(0, 0)

    @pl.when(b + 1 < nb)
    def _():
        start(b + 1, 1 - slot)

    for c in _page_copies(kpool_ref, pt_ref, kslab, ksem, b, slot, n_pages):
        c.wait()
    for c in _page_copies(vpool_ref, pt_ref, vslab, vsem, b, slot, n_pages):
        c.wait()

    kc_ref[0] = _compress_core(lambda p: kslab[slot, pl.ds(p, nrows, stride=CMP_STRIDE), :],
                               pek_ref, wtk_ref, wbk_ref, w2k_ref, nrows)
    vc_ref[0] = _compress_core(lambda p: vslab[slot, pl.ds(p, nrows, stride=CMP_STRIDE), :],
                               pev_ref, wtv_ref, wbv_ref, w2v_ref, nrows)


def _compress_sample(page_table, kpool, vpool, cw_k, cw_v):
    bsz, n_pages = page_table.shape
    L = n_pages * PAGE_SIZE
    nrows = L // CMP_STRIDE
    const = lambda a: pl.BlockSpec(a.shape, lambda b, pt: (0,) * a.ndim)
    anyspec = pl.BlockSpec(memory_space=pl.ANY)
    out = pl.BlockSpec((1, nrows, KV_W), lambda b, pt: (b, 0, 0))
    grid_spec = pltpu.PrefetchScalarGridSpec(
        num_scalar_prefetch=1,
        grid=(bsz,),
        in_specs=[anyspec, anyspec] + [const(a) for a in cw_k + cw_v],
        out_specs=[out, out],
        scratch_shapes=[pltpu.VMEM((2, L, KV_W), F32), pltpu.VMEM((2, L, KV_W), F32),
                        pltpu.SemaphoreType.DMA((2,)), pltpu.SemaphoreType.DMA((2,))],
    )
    return pl.pallas_call(
        _compress_sample_kernel,
        grid_spec=grid_spec,
        out_shape=[jax.ShapeDtypeStruct((bsz, nrows, KV_W), F32)] * 2,
        compiler_params=_cparams(1),
        name="compress_sample",
    )(page_table, kpool, vpool, *cw_k, *cw_v)


SLC_CHUNK = 512


def _attend_prompt_kernel(q_ref, g_ref, kc_ref, vc_ref, ks_ref, vs_ref, kw_ref, vw_ref, m_ref,
                          o_ref):
    i = pl.program_id(1)
    qb = QUERY_BLOCK
    rows = GROUP * qb
    s0 = i * qb
    seq = ks_ref.shape[1]
    n_cmp = kc_ref.shape[1]
    n_sel = seq // SEL_BLOCK
    lane = lax.broadcasted_iota(jnp.int32, (1, LANES), 1)
    t_q = s0 + lax.broadcasted_iota(jnp.int32, (qb, 1), 0)
    t_row = jnp.concatenate([t_q] * GROUP, axis=0)
    q = q_ref[...] * (HEAD_DIM ** -0.5)
    kc = kc_ref[0].astype(BF16)
    vc = vc_ref[0].astype(BF16)
    c_end = lax.broadcasted_iota(jnp.int32, (1, n_cmp), 1) * CMP_STRIDE + (CMP_BLOCK - 1)
    c_mask = c_end <= t_row

    w_start = pl.multiple_of(jnp.maximum(s0 - WINDOW, 0), qb)
    w_len = WINDOW + qb
    kwin = kw_ref[0, pl.ds(w_start, w_len), :].astype(BF16)
    vwin = vw_ref[0, pl.ds(w_start, w_len), :].astype(BF16)
    dt = t_row - (w_start + lax.broadcasted_iota(jnp.int32, (1, w_len), 1))
    w_mask = (dt >= 0) & (dt < WINDOW)

    n_chunks = lax.shift_right_logical(s0 + qb - 1, 9) + 1
    o_c, o_s, o_w = [], [], []
    for h in range(N_KV_HEADS):
        head_lanes = (lane >= h * HEAD_DIM) & (lane < (h + 1) * HEAD_DIM)
        qp = jnp.concatenate(
            [jnp.where(head_lanes, q[:, g * LANES:(g + 1) * LANES], 0.0) for g in range(GROUP)],
            axis=0).astype(BF16)
        (p_c,) = _masked_softmax_parts([(_dot_nt(qp, kc), c_mask)])
        o_c.append(_dot(p_c.astype(BF16), vc))
        p_sum = p_c[0:qb]
        for g in range(1, GROUP):
            p_sum = p_sum + p_c[g * qb:(g + 1) * qb]
        imp = _importance(_dot_exact_lhs(p_sum, m_ref[...]), t_q, n_sel)
        sel = _topk_select(imp, n_sel, N_SEL).astype(BF16)

        def body(c, carry):
            m_i, l_i, acc = carry
            k0 = pl.multiple_of(c * SLC_CHUNK, SLC_CHUNK)
            k_c = ks_ref[0, pl.ds(k0, SLC_CHUNK), :].astype(BF16)
            v_c = vs_ref[0, pl.ds(k0, SLC_CHUNK), :].astype(BF16)
            kpos = k0 + lax.broadcasted_iota(jnp.int32, (1, SLC_CHUNK), 1)
            blk = lax.shift_right_logical(kpos, 6)
            expand = jnp.where(lax.broadcasted_iota(jnp.int32, (LANES, SLC_CHUNK), 0) == blk,
                               1.0, 0.0).astype(BF16)
            picked = (_dot(sel, expand) > 0.5) & (kpos <= t_q)
            mask = jnp.concatenate([picked] * GROUP, axis=0)
            s = jnp.where(mask, _dot_nt(qp, k_c), NEG_BIG)
            m_new = jnp.maximum(m_i, jnp.max(s, axis=-1, keepdims=True))
            p = jnp.exp(s - m_new)
            alpha = jnp.exp(m_i - m_new)
            l_new = alpha * l_i + jnp.sum(p, axis=-1, keepdims=True)
            acc_new = alpha * acc + _dot(p.astype(BF16), v_c)
            return m_new, l_new, acc_new

        init = (jnp.full((rows, 1), NEG_BIG, F32), jnp.zeros((rows, 1), F32),
                jnp.zeros((rows, LANES), F32))
        _, l_s, acc_s = lax.fori_loop(0, n_chunks, body, init)
        o_s.append(acc_s / jnp.maximum(l_s, 1e-30))

        (p_w,) = _masked_softmax_parts([(_dot_nt(qp, kwin), w_mask)])
        o_w.append(_dot(p_w.astype(BF16), vwin))

    low = lane < HEAD_DIM
    gates = g_ref[...]
    branches = [jnp.where(low, o[0], o[1]) for o in (o_c, o_s, o_w)]
    for g in range(GROUP):
        acc = jnp.zeros((qb, LANES), F32)
        for br in range(3):
            c0 = g * 3 + br
            c1 = GROUP * 3 + g * 3 + br
            gate = jnp.where(low, jnp.broadcast_to(gates[:, c0:c0 + 1], (qb, LANES)),
                             jnp.broadcast_to(gates[:, c1:c1 + 1], (qb, LANES)))
            acc = acc + gate * branches[br][g * qb:(g + 1) * qb]
        o_ref[:, g * LANES:(g + 1) * LANES] = acc


def _attend_prompt(q, gates, kc, vc, ks, vs, kw, vw, m_sel):
    bsz, seq, _ = ks.shape
    nqb = seq // QUERY_BLOCK
    row = lambda w: pl.BlockSpec((QUERY_BLOCK, w), lambda b, i: (b * nqb + i, 0))
    per_b = lambda a: pl.BlockSpec((1,) + a.shape[1:], lambda b, i: (b, 0, 0))
    const = lambda a: pl.BlockSpec(a.shape, lambda b, i: (0,) * a.ndim)
    return pl.pallas_call(
        _attend_prompt_kernel,
        grid=(bsz, nqb),
        in_specs=[row(Q_W), row(LANES), per_b(kc), per_b(vc), per_b(ks), per_b(vs),
                  per_b(kw), per_b(vw), const(m_sel)],
        out_specs=row(Q_W),
        out_shape=jax.ShapeDtypeStruct((bsz * seq, Q_W), F32),
        compiler_params=_cparams(2),
        name="attend_prompt",
    )(q, gates, kc, vc, ks, vs, kw, vw, m_sel)


def _attend_sample_kernel(pt_ref, q_ref, g_ref, kc_ref, vc_ref, kpool_ref, vpool_ref,
                          ksn_ref, vsn_ref, kwc_ref, vwc_ref, kwn_ref, vwn_ref,
                          m_ref, e_ref, r_ref, o_ref, kslab, vslab, ksem, vsem):
    b = pl.program_id(0)
    nb = pl.num_programs(0)
    n_pages = pt_ref.shape[1]
    past = n_pages * PAGE_SIZE
    slot = b % 2

    def start(bb, sl):
        for c in _page_copies(kpool_ref, pt_ref, kslab, ksem, bb, sl, n_pages):
            c.start()
        for c in _page_copies(vpool_ref, pt_ref, vslab, vsem, bb, sl, n_pages):
            c.start()

    @pl.when(b == 0)
    def _():
        start(0, 0)

    @pl.when(b + 1 < nb)
    def _():
        start(b + 1, 1 - slot)

    rows = q_ref.shape[1]
    n_cmp = kc_ref.shape[1]
    n_sel = (past + DEC_SEQ + SEL_BLOCK - 1) // SEL_BLOCK
    sel_lanes = m_ref.shape[1]
    r_id = lax.broadcasted_iota(jnp.int32, (rows, 1), 0)
    t_row = past + (r_id & (DEC_SEQ - 1))
    qp = (q_ref[0] * (HEAD_DIM ** -0.5)).astype(BF16)

    c_end = lax.broadcasted_iota(jnp.int32, (1, n_cmp), 1) * CMP_STRIDE + (CMP_BLOCK - 1)
    (p_c,) = _masked_softmax_parts([(_dot_nt(qp, kc_ref[0].astype(BF16)), c_end <= t_row)])
    o_c = _dot(p_c.astype(BF16), vc_ref[0].astype(BF16))
    p_sum = _dot_exact_rhs(r_ref[...], p_c)
    imp = _importance(_dot_exact_lhs(p_sum, m_ref[...]), t_row, n_sel)
    sel = _topk_select(imp, n_sel, N_SEL).astype(BF16)

    n_win = kwc_ref.shape[1]
    n_new = kwn_ref.shape[1]
    pos_c = past - n_win + lax.broadcasted_iota(jnp.int32, (1, n_win), 1)
    dt_c = t_row - pos_c
    new_id = lax.broadcasted_iota(jnp.int32, (1, n_new), 1)
    new_ok = (past + new_id <= t_row) & (new_id < DEC_SEQ)
    p_wc, p_wn = _masked_softmax_parts([
        (_dot_nt(qp, kwc_ref[0].astype(BF16)), (dt_c >= 0) & (dt_c < WINDOW) & (pos_c >= 0)),
        (_dot_nt(qp, kwn_ref[0].astype(BF16)), new_ok)])
    o_w = (_dot(p_wc.astype(BF16), vwc_ref[0].astype(BF16))
           + _dot(p_wn.astype(BF16), vwn_ref[0].astype(BF16)))

    for c in _page_copies(kpool_ref, pt_ref, kslab, ksem, b, slot, n_pages):
        c.wait()
    for c in _page_copies(vpool_ref, pt_ref, vslab, vsem, b, slot, n_pages):
        c.wait()
    picked = _dot(sel, e_ref[...]) > 0.5
    kpos = lax.broadcasted_iota(jnp.int32, (1, past), 1)
    p_sp, p_sn = _masked_softmax_parts([
        (_dot_nt(qp, kslab[slot].astype(BF16)), picked[:, :past] & (kpos <= t_row)),
        (_dot_nt(qp, ksn_ref[0].astype(BF16)), picked[:, past:] & new_ok)])
    o_s = (_dot(p_sp.astype(BF16), vslab[slot].astype(BF16))
           + _dot(p_sn.astype(BF16), vsn_ref[0].astype(BF16)))

    gates = g_ref[0]
    bc = lambda c: jnp.broadcast_to(gates[:, c:c + 1], (rows, LANES))
    o_ref[0] = bc(0) * o_c + bc(1) * o_s + bc(2) * o_w


def _attend_sample(page_table, qp, gates, kc, vc, kpool, vpool, ks_new, vs_new,
                   kw_cache, vw_cache, kw_new, vw_new, m_sel, e_sel, r_sum):
    bsz, n_pages = page_table.shape
    past = n_pages * PAGE_SIZE
    per_b = lambda a: pl.BlockSpec((1,) + a.shape[1:], lambda b, pt: (b, 0, 0))
    const = lambda a: pl.BlockSpec(a.shape, lambda b, pt: (0,) * a.ndim)
    anyspec = pl.BlockSpec(memory_space=pl.ANY)
    grid_spec = pltpu.PrefetchScalarGridSpec(
        num_scalar_prefetch=1,
        grid=(bsz,),
        in_specs=[per_b(qp), per_b(gates), per_b(kc), per_b(vc), anyspec, anyspec,
                  per_b(ks_new), per_b(vs_new), per_b(kw_cache), per_b(vw_cache),
                  per_b(kw_new), per_b(vw_new), const(m_sel), const(e_sel), const(r_sum)],
        out_specs=per_b(qp),
        scratch_shapes=[pltpu.VMEM((2, past, KV_W), F32), pltpu.VMEM((2, past, KV_W), F32),
                        pltpu.SemaphoreType.DMA((2,)), pltpu.SemaphoreType.DMA((2,))],
    )
    return pl.pallas_call(
        _attend_sample_kernel,
        grid_spec=grid_spec,
        out_shape=jax.ShapeDtypeStruct(qp.shape, F32),
        compiler_params=_cparams(1),
        name="attend_sample",
    )(page_table, qp, gates, kc, vc, kpool, vpool, ks_new, vs_new,
      kw_cache, vw_cache, kw_new, vw_new, m_sel, e_sel, r_sum)


def _memkv_kernel(m_ref, g_ref, wk_ref, wv_ref, k_ref, v_ref):
    h = _rms(m_ref[...], g_ref[...]).astype(BF16)
    k_ref[...] = _dot(h, wk_ref[...])
    v_ref[...] = _dot(h, wv_ref[...])


def _memkv(mem, gain, wk, wv):
    n = mem.shape[0]
    tm = N_MEM
    row = lambda w: pl.BlockSpec((tm, w), lambda i: (i, 0))
    const = lambda a: pl.BlockSpec(a.shape, lambda i: (0,) * a.ndim)
    return pl.pallas_call(
        _memkv_kernel,
        grid=(n // tm,),
        in_specs=[row(D_MODEL), const(gain), const(wk), const(wv)],
        out_specs=[row(X_W), row(X_W)],
        out_shape=[jax.ShapeDtypeStruct((n, X_W), F32)] * 2,
        compiler_params=_cparams(1),
        name="mem_kv",
    )(mem, gain, wk, wv)


CONV_HALO = 32


def _mix_kernel(*refs, conv_taps_given, tiles_per_seq, rows_per_batch):
    if conv_taps_given:
        (x_ref, o_ref, taps_ref, mk_ref, mv_ref, gmix_ref, wmerge_ref, wnsa_ref, wdw_ref, bdw_ref,
         lng_ref, lnb_ref, wpw_ref, wout_ref, gx_ref, wxq_ref, wxo_ref, out_ref) = refs
    else:
        (x_ref, o_ref, glu_ref, prev_ref, mk_ref, mv_ref, gmix_ref, wmerge_ref, wnsa_ref, wdw_ref,
         bdw_ref, lng_ref, lnb_ref, wpw_ref, wout_ref, gx_ref, wxq_ref, wxo_ref, out_ref,
         u_ref) = refs
    x = x_ref[...]
    tm = x.shape[0]
    wdw = wdw_ref[...]

    c = jnp.broadcast_to(bdw_ref[...], (tm, CONV_CH))
    if conv_taps_given:
        for w in range(CONV_WIDTH):
            c = c + taps_ref[w] * wdw[w:w + 1, :]
    else:
        first = (pl.program_id(0) % tiles_per_seq) == 0
        u_ref[0:CONV_HALO, :] = jnp.where(first, 0.0, prev_ref[...])
        u_ref[CONV_HALO:CONV_HALO + tm, :] = glu_ref[...]
        off = CONV_HALO - (CONV_WIDTH - 1)
        for w in range(CONV_WIDTH):
            c = c + u_ref[off + w:off + w + tm, :] * wdw[w:w + 1, :]
    mu = jnp.mean(c, axis=-1, keepdims=True)
    cc = c - mu
    var = jnp.mean(cc * cc, axis=-1, keepdims=True)
    cn = cc * lax.rsqrt(var + EPS) * lng_ref[...] + lnb_ref[...]
    y_conv = _dot(jax.nn.silu(cn).astype(BF16), wpw_ref[...])

    y_nsa = _dot(o_ref[...].astype(BF16), wnsa_ref[...])
    h = _rms(x, gmix_ref[...]).astype(BF16)
    gm = jax.nn.sigmoid(_dot(h, wmerge_ref[...]))
    mix = gm[:, :D_MODEL] * y_nsa + gm[:, D_MODEL:] * y_conv
    x1 = x + _dot(mix.astype(BF16), wout_ref[...])

    hx = _rms(x1, gx_ref[...]).astype(BF16)
    qx = _dot(hx, wxq_ref[...])
    nb = mk_ref.shape[0]
    mk = mk_ref[...].reshape(nb * N_MEM, X_W)
    mv = mv_ref[...].reshape(nb * N_MEM, X_W)
    if nb > 1:
        r_b = lax.broadcasted_iota(jnp.int32, (tm, 1), 0) // rows_per_batch
        k_b = lax.broadcasted_iota(jnp.int32, (1, nb * N_MEM), 1) // N_MEM
        same = r_b == k_b
    ctx = []
    for hd in range(X_HEADS):
        sl = slice(hd * X_HEAD_DIM, (hd + 1) * X_HEAD_DIM)
        s = _dot_nt(qx[:, sl].astype(BF16), mk[:, sl].astype(BF16)) * (X_HEAD_DIM ** -0.5)
        if nb > 1:
            s = jnp.where(same, s, NEG_BIG)
        s = s - jnp.max(s, axis=-1, keepdims=True)
        e = jnp.exp(s)
        p = e / jnp.sum(e, axis=-1, keepdims=True)
        ctx.append(_dot(p.astype(BF16), mv[:, sl].astype(BF16)))
    ctx = jnp.concatenate(ctx, axis=-1)
    out_ref[...] = x1 + _dot(ctx.astype(BF16), wxo_ref[...])


def _mix(x, o, conv_in, mk, mv, params, tm, nb, tiles_per_seq, rows_per_batch):
    n = x.shape[0]
    row = lambda w: pl.BlockSpec((tm, w), lambda i: (i, 0))
    const = lambda a: pl.BlockSpec(a.shape, lambda i: (0,) * a.ndim)
    mem = pl.BlockSpec((nb, N_MEM, X_W), lambda i: (i // tiles_per_seq, 0, 0))
    taps_given = conv_in[0] == "taps"
    if taps_given:
        conv_specs = [pl.BlockSpec((CONV_WIDTH, tm, CONV_CH), lambda i: (0, i, 0))]
        conv_args = [conv_in[1]]
        scratch = []
    else:
        halo_blocks = tm // CONV_HALO
        conv_specs = [row(CONV_CH),
                      pl.BlockSpec((CONV_HALO, CONV_CH),
                                   lambda i: (jnp.maximum(i * halo_blocks - 1, 0), 0))]
        conv_args = [conv_in[1], conv_in[1]]
        scratch = [pltpu.VMEM((CONV_HALO + tm, CONV_CH), F32)]
    kern = functools.partial(_mix_kernel, conv_taps_given=taps_given,
                             tiles_per_seq=tiles_per_seq, rows_per_batch=rows_per_batch)
    return pl.pallas_call(
        kern,
        grid=(n // tm,),
        in_specs=[row(D_MODEL), row(Q_W)] + conv_specs + [mem, mem] + [const(a) for a in params],
        out_specs=row(D_MODEL),
        out_shape=jax.ShapeDtypeStruct((n, D_MODEL), F32),
        scratch_shapes=scratch,
        compiler_params=_cparams(1),
        name="mix_xattn",
    )(x, o, *conv_args, mk, mv, *params)


FF_CHUNK = 1024


def _mlp_kernel(x_ref, gff_ref, wup_ref, wdown_ref, gfin_ref, y_ref):
    x = x_ref[...]
    h = _rms(x, gff_ref[...]).astype(BF16)
    acc = x
    for c in range(D_FF // FF_CHUNK):
        u = jnp.maximum(_dot(h, wup_ref[:, c * FF_CHUNK:(c + 1) * FF_CHUNK]), 0.0)
        acc = acc + _dot((u * u).astype(BF16), wdown_ref[c * FF_CHUNK:(c + 1) * FF_CHUNK, :])
    y_ref[...] = _rms(acc, gfin_ref[...])


def _mlp(x, gff, wup, wdown, gfin, tm):
    n = x.shape[0]
    row = pl.BlockSpec((tm, D_MODEL), lambda i: (i, 0))
    const = lambda a: pl.BlockSpec(a.shape, lambda i: (0,) * a.ndim)
    return pl.pallas_call(
        _mlp_kernel,
        grid=(n // tm,),
        in_specs=[row, const(gff), const(wup), const(wdown), const(gfin)],
        out_specs=row,
        out_shape=jax.ShapeDtypeStruct((n, D_MODEL), F32),
        compiler_params=_cparams(1),
        name="mlp_final",
    )(x, gff, wup, wdown, gfin)


def _rope_tables(pos):
    half = ROT_DIM // 2
    inv = ROPE_THETA ** (-jnp.arange(half, dtype=F32) / half)
    ang = pos.astype(F32)[:, None] * inv[None, :]
    cos, sin = jnp.cos(ang), jnp.sin(ang)
    n = pos.shape[0]
    rest = HEAD_DIM - ROT_DIM
    head = lambda a, b, fill: jnp.concatenate([a, b, jnp.full((n, rest), fill, F32)], axis=-1)
    zero = jnp.zeros_like(sin)
    cos_t = head(cos, cos, 1.0)
    sa_t = head(-sin, zero, 0.0)
    sb_t = head(zero, sin, 0.0)
    rep = LANES // HEAD_DIM
    return tuple(jnp.tile(t, (1, rep)) for t in (cos_t, sa_t, sb_t))


def _compress_weights(pos_emb, w1, w2):
    half = CMP_BLOCK // 2
    pe = jnp.tile(pos_emb, (1, N_KV_HEADS))
    w1r = w1.reshape(CMP_BLOCK, HEAD_DIM, CMP_HIDDEN)
    zero = jnp.zeros_like(w1r)
    wbd = jnp.concatenate([jnp.concatenate([w1r, zero], axis=2),
                           jnp.concatenate([zero, w1r], axis=2)], axis=1).astype(BF16)
    z2 = jnp.zeros_like(w2)
    w2bd = jnp.concatenate([jnp.concatenate([w2, z2], axis=1),
                            jnp.concatenate([z2, w2], axis=1)], axis=0).astype(BF16)
    return [pe, wbd[:half], wbd[half:], w2bd]


def _cmp_to_sel(n_cmp_rows, n_sel_lanes, nc, ns):
    start = np.arange(nc)[:, None] * CMP_STRIDE
    sel0 = np.arange(ns)[None, :] * SEL_BLOCK
    ov = np.clip(np.minimum(start + CMP_BLOCK, sel0 + SEL_BLOCK) - np.maximum(start, sel0), 0, None)
    m = np.zeros((n_cmp_rows, n_sel_lanes), np.float32)
    m[:nc, :ns] = ov / CMP_BLOCK
    return jnp.asarray(m, dtype=BF16)


def _q_perm():
    g, h, d = np.meshgrid(np.arange(GROUP), np.arange(N_KV_HEADS), np.arange(HEAD_DIM), indexing="ij")
    return ((h * GROUP + g) * HEAD_DIM + d).reshape(-1)


def kernel(x_prompt, x_sample, mem_prompt, cache_k_cmp, cache_v_cmp, cache_k_slc, cache_v_slc, cache_k_win, cache_v_win, state_conv, cache_mem_k, cache_mem_v, page_table, norm_mix, w_in, cmp_pos_k, cmp_pos_v, w_ck1, w_ck2, w_cv1, w_cv2, w_nsa_o, w_dw, b_dw, conv_ln_g, conv_ln_b, w_pw, w_out, norm_x, norm_mem, w_xq, w_xk, w_xv, w_xo, norm_ff, w_up, w_down, norm_final):
    assert norm_mix.shape[0] == 1, "single layer"
    bsz, seq, _ = x_prompt.shape
    dbsz, dseq, _ = x_sample.shape
    n_pages = page_table.shape[1]
    past = n_pages * PAGE_SIZE
    assert seq == SEQ and dseq == DEC_SEQ and past == PAST_LEN

    perm = _q_perm()
    win = w_in[0]
    o = Q_W
    w_q = win[:, :Q_W][:, perm]
    w_kv = win[:, o:o + 6 * KV_W]
    o += 6 * KV_W
    w_gate = jnp.pad(win[:, o:o + 3 * N_HEADS], ((0, 0), (0, LANES - 3 * N_HEADS)))
    o += 3 * N_HEADS
    w_glu = win[:, o:o + 2 * CONV_CH]
    o += 2 * CONV_CH
    w_merge = win[:, o:].astype(BF16)
    w_proj = jnp.concatenate([w_q, w_kv, w_gate, w_glu], axis=1).astype(BF16)
    row2 = lambda a: a.reshape(1, -1)
    cw_k = _compress_weights(cmp_pos_k[0], w_ck1[0], w_ck2[0])
    cw_v = _compress_weights(cmp_pos_v[0], w_cv1[0], w_cv2[0])
    wdw = jnp.pad(w_dw[0], ((0, CONV_HALO - CONV_WIDTH), (0, 0)))
    mix_params = [row2(norm_mix[0]), w_merge, w_nsa_o[0][perm, :].astype(BF16), wdw, row2(b_dw[0]),
                  row2(conv_ln_g[0]), row2(conv_ln_b[0]), w_pw[0].astype(BF16),
                  w_out[0].astype(BF16), row2(norm_x[0]), w_xq[0].astype(BF16),
                  w_xo[0].astype(BF16)]
    wup = w_up[0].astype(BF16)
    wdown = w_down[0].astype(BF16)
    gff = row2(norm_ff[0])
    gfin = row2(norm_final)
    gmix = row2(norm_mix[0])

    n_p = bsz * seq
    xp = x_prompt.reshape(n_p, D_MODEL)
    tm_p = 512
    cos, sa, sb = _rope_tables(jnp.arange(seq, dtype=jnp.int32))
    (q, k_cmp, v_cmp, k_slc, v_slc, k_win, v_win, gates, glu) = _proj(
        xp, gmix, w_proj, cos, sa, sb, tm_p, seq // tm_p)
    r3 = lambda a: a.reshape(bsz, seq, KV_W)
    kc, vc = _compress_prompt(r3(k_cmp), r3(v_cmp), cw_k, cw_v)
    nc = (seq - CMP_BLOCK) // CMP_STRIDE + 1
    m_sel = _cmp_to_sel(seq // CMP_STRIDE, LANES, nc, seq // SEL_BLOCK)
    o_p = _attend_prompt(q, gates, kc, vc, r3(k_slc), r3(v_slc), r3(k_win), r3(v_win), m_sel)
    mk, mv = _memkv(mem_prompt.reshape(bsz * N_MEM, D_MODEL), row2(norm_mem[0]),
                    w_xk[0].astype(BF16), w_xv[0].astype(BF16))
    mk3 = mk.reshape(bsz, N_MEM, X_W)
    mv3 = mv.reshape(bsz, N_MEM, X_W)
    tm_mix = 256
    x2 = _mix(xp, o_p, ("glu", glu), mk3, mv3, mix_params, tm_mix, 1, seq // tm_mix, seq)
    y_prompt = _mlp(x2, gff, wup, wdown, gfin, tm_p).reshape(bsz, seq, D_MODEL)

    kv5 = lambda a: a.reshape(1, bsz, seq, N_KV_HEADS, HEAD_DIM)
    wb = min(WINDOW, seq)
    p_state = (kv5(k_cmp), kv5(v_cmp), kv5(k_slc), kv5(v_slc),
               kv5(k_win)[:, :, seq - wb:], kv5(v_win)[:, :, seq - wb:],
               glu.reshape(1, bsz, seq, CONV_CH)[:, :, seq - (CONV_WIDTH - 1):],
               mk.reshape(1, bsz, N_MEM, X_HEADS, X_HEAD_DIM),
               mv.reshape(1, bsz, N_MEM, X_HEADS, X_HEAD_DIM))

    n_s = dbsz * dseq
    xs = x_sample.reshape(n_s, D_MODEL)
    cos_s, sa_s, sb_s = _rope_tables(past + jnp.arange(dseq, dtype=jnp.int32))
    tile_s = lambda t: jnp.tile(t, (dbsz, 1))
    (q_s, k_cmp_s, v_cmp_s, k_slc_s, v_slc_s, k_win_s, v_win_s, gates_s, glu_s) = _proj(
        xs, gmix, w_proj, tile_s(cos_s), tile_s(sa_s), tile_s(sb_s), n_s, 1)
    pool = lambda a: a[0].reshape(a.shape[1], PAGE_SIZE, KV_W)
    kc_s, vc_s = _compress_sample(page_table, pool(cache_k_cmp), pool(cache_v_cmp), cw_k, cw_v)

    q5 = q_s.reshape(dbsz, dseq, GROUP, N_KV_HEADS, HEAD_DIM).transpose(0, 3, 2, 1, 4)
    zq = jnp.zeros_like(q5[:, 0])
    qp_s = jnp.stack([jnp.concatenate([q5[:, 0], zq], axis=-1),
                      jnp.concatenate([zq, q5[:, 1]], axis=-1)], axis=1)
    rows_s = N_KV_HEADS * GROUP * dseq
    qp_s = qp_s.reshape(dbsz, rows_s, LANES)
    g5 = gates_s[:, :3 * N_HEADS].reshape(dbsz, dseq, N_KV_HEADS, GROUP, 3).transpose(0, 2, 3, 1, 4)
    g_s = jnp.pad(g5.reshape(dbsz, rows_s, 3), ((0, 0), (0, 0), (0, LANES - 3)))
    new_blk = lambda a: jnp.pad(a.reshape(dbsz, dseq, KV_W), ((0, 0), (0, LANES - dseq), (0, 0)))
    n_win = cache_k_win.shape[2]
    wcache = lambda a: a[0].reshape(dbsz, n_win, KV_W)
    nc_s = (past + dseq - CMP_BLOCK) // CMP_STRIDE + 1
    ns_s = -(-(past + dseq) // SEL_BLOCK)
    sel_lanes = 2 * LANES
    m_sel_s = _cmp_to_sel(past // CMP_STRIDE, sel_lanes, nc_s, ns_s)
    key_blk = np.arange(past + LANES) // SEL_BLOCK
    e_sel = jnp.asarray(np.arange(sel_lanes)[:, None] == key_blk[None, :], dtype=BF16)
    rid = np.arange(rows_s)
    same_head_step = ((rid[:, None] // (GROUP * dseq) == rid[None, :] // (GROUP * dseq))
                      & (rid[:, None] % dseq == rid[None, :] % dseq))
    r_sum = jnp.asarray(same_head_step, dtype=BF16)
    o32 = _attend_sample(page_table, qp_s, g_s, kc_s, vc_s, pool(cache_k_slc), pool(cache_v_slc),
                         new_blk(k_slc_s), new_blk(v_slc_s), wcache(cache_k_win),
                         wcache(cache_v_win), new_blk(k_win_s), new_blk(v_win_s),
                         m_sel_s, e_sel, r_sum)
    o6 = o32.reshape(dbsz, N_KV_HEADS, GROUP, dseq, N_KV_HEADS, HEAD_DIM)
    o_s = jnp.stack([o6[:, 0, :, :, 0], o6[:, 1, :, :, 1]], axis=3)
    o_s = o_s.transpose(0, 2, 1, 3, 4).reshape(n_s, Q_W)

    u_s = jnp.concatenate([state_conv[0], glu_s.reshape(dbsz, dseq, CONV_CH)], axis=1)
    taps = jnp.stack([u_s[:, w:w + dseq] for w in range(CONV_WIDTH)], axis=0)
    taps = taps.reshape(CONV_WIDTH, n_s, CONV_CH)
    nb_s = 8
    cmk = cache_mem_k[0].reshape(dbsz, N_MEM, X_W)
    cmv = cache_mem_v[0].reshape(dbsz, N_MEM, X_W)
    x2_s = _mix(xs, o_s, ("taps", taps), cmk, cmv, mix_params, nb_s * dseq, nb_s, 1, dseq)
    y_sample = _mlp(x2_s, gff, wup, wdown, gfin, n_s).reshape(dbsz, dseq, D_MODEL)

    kvs = lambda a: a.reshape(1, dbsz, dseq, N_KV_HEADS, HEAD_DIM)
    s_state = (kvs(k_cmp_s), kvs(v_cmp_s), kvs(k_slc_s), kvs(v_slc_s),
               jnp.concatenate([cache_k_win, kvs(k_win_s)], axis=2)[:, :, dseq:],
               jnp.concatenate([cache_v_win, kvs(v_win_s)], axis=2)[:, :, dseq:],
               u_s[None, :, dseq:])
    return (y_prompt, y_sample) + p_state + s_state
```
